```python
import math
import jax, jax.numpy as jnp
from jax import lax
import numpy as np

D_MODEL = 1024
BATCH = 32
SEQ = 256
DEPTH = 2
DEC_BATCH = 4
DEC_SEQ = 4096
PAST_LEN = 256

F32 = jnp.float32
GRID_W = 64
N_EVEN = (DEPTH + 1) // 2
N_ODD = DEPTH // 2
N_MOD = 9
D_FF = 2816
FFN_RES = 0.5
EPS = 1e-6
H_A = 4
DK_A = 128
DV_A = 128
CONV_W = 5
GDN_CHUNK = 64
H_B = 4
Q_LORA = 256
KV_LORA = 128
DN_B = 128
DR_B = 64
DV_B = 128
ROPE_BASE = 10000.0
Q_BLOCK = 128
H_C = 4
DK_C = 128
DV_C = 256
GATE_RANK = 16
GATE_NORM = 16.0
GLA_CHUNK = 32
EV_SIZES = (2 * H_A * DK_A + H_A * DV_A, H_A * DV_A, 2 * H_A, 2 * H_A, Q_LORA, KV_LORA, DR_B)
EV_IN = 2 * H_A * DK_A + 2 * H_A * DV_A + 4 * H_A + Q_LORA + KV_LORA + DR_B
EV_MIX = H_A * DV_A + H_B * DV_B
OD_SIZES = (H_C * DK_C, H_C * DK_C, H_C * DV_C, H_C * DV_C, 2 * GATE_RANK)
OD_IN = 2 * H_C * DK_C + 2 * H_C * DV_C + 2 * GATE_RANK
OD_MIX = H_C * DV_C

kernel_name = 'bidir_gdn_mla_gla_prefix_trunk'


def _split(x, sizes):
    out, off = [], 0
    for s in sizes:
        out.append(x[..., off:off + s])
        off += s
    return out


def rmsnorm(x, w):
    x32 = x.astype(F32)
    y = x32 * lax.rsqrt(jnp.mean(x32 * x32, axis=-1, keepdims=True) + EPS)
    return (y * w.astype(F32)).astype(x.dtype)


def l2norm(x):
    x32 = x.astype(F32)
    return (x32 * lax.rsqrt(jnp.sum(x32 * x32, axis=-1, keepdims=True) + EPS)).astype(x.dtype)


def adaln(cond, w, b):
    m = (jax.nn.silu(cond) @ w + b).reshape(cond.shape[0], 1, N_MOD, D_MODEL)
    return [m[:, :, j] for j in range(N_MOD)]


def modulate(x, g, shift, scale):
    return rmsnorm(x, g) * (1 + scale) + shift


def swiglu(h, w_in, w_out):
    u, v = jnp.split(h @ w_in, 2, axis=-1)
    return (jax.nn.silu(u) * v) @ w_out


def ffn_branch(x, mods, g_pre, g_post, w_in, w_out):
    shift, scale, gate = mods
    y = swiglu(modulate(x, g_pre, shift, scale), w_in, w_out)
    return x + FFN_RES * gate * rmsnorm(y, g_post)


def short_conv(x, w):
    return lax.conv_general_dilated(x, w.astype(x.dtype)[:, None, :], window_strides=(1,),
                                    padding=((CONV_W // 2, CONV_W // 2),),
                                    dimension_numbers=('NWC', 'WIO', 'NWC'),
                                    feature_group_count=x.shape[-1])


def grid_rope(n_tokens):
    rows = n_tokens // GRID_W
    row = jnp.repeat(jnp.arange(rows, dtype=F32), GRID_W)
    col = jnp.tile(jnp.arange(GRID_W, dtype=F32), rows)
    nf = DR_B // 4
    inv = ROPE_BASE ** (-jnp.arange(nf, dtype=F32) / nf)
    ang = jnp.stack([row[:, None] * inv, col[:, None] * inv], axis=1)
    return jnp.cos(ang), jnp.sin(ang)


def rope_2d(x, cos, sin):
    B, T, H, R = x.shape
    xr = x.reshape(B, T, H, 2, 2, R // 4)
    x1, x2 = xr[..., 0, :], xr[..., 1, :]
    c = cos[None, :, None].astype(x.dtype)
    s = sin[None, :, None].astype(x.dtype)
    return jnp.stack([x1 * c - x2 * s, x2 * c + x1 * s], axis=-2).reshape(B, T, H, R)


def _to_chunks(x, c):
    B, T = x.shape[:2]
    x = x.astype(F32).reshape((B, T // c, c) + x.shape[2:])
    return jnp.swapaxes(jnp.moveaxis(x, 1, 0), 2, 3)


def _from_chunks(o):
    n, B, H, c, E = o.shape
    return jnp.moveaxis(jnp.swapaxes(o, 2, 3), 0, 1).reshape(B, n * c, H, E)


def gdn_chunk_scan(q, k, v, g, beta, s0):
    C = GDN_CHUNK
    DV = v.shape[-1]
    qc, kc, vc = _to_chunks(q, C), _to_chunks(k, C), _to_chunks(v, C)
    gc = jnp.cumsum(_to_chunks(g, C), axis=-1)
    bc = _to_chunks(beta, C)
    idx = jnp.arange(C)
    causal = idx[:, None] >= idx[None, :]
    strict = idx[:, None] > idx[None, :]
    decay = jnp.exp(jnp.where(causal, gc[..., :, None] - gc[..., None, :], -jnp.inf))
    kb = kc * bc[..., None]
    a_mat = jnp.where(strict, jnp.einsum('nbhik,nbhjk->nbhij', kb, kc) * decay, 0.0)
    rhs = jnp.concatenate([vc * bc[..., None], kb * jnp.exp(gc)[..., None]], axis=-1)
    sol = lax.linalg.triangular_solve(a_mat + jnp.eye(C, dtype=F32), rhs, left_side=True, lower=True)
    u_all, w_all = sol[..., :DV], sol[..., DV:]
    qk = jnp.einsum('nbhik,nbhjk->nbhij', qc, kc) * decay

    def step(S, inp):
        q_i, k_i, u_i, w_i, qk_i, g_i = inp
        v_new = u_i - jnp.einsum('bhck,bhkv->bhcv', w_i, S)
        o = (jnp.einsum('bhck,bhkv->bhcv', q_i * jnp.exp(g_i)[..., None], S)
             + jnp.einsum('bhij,bhjv->bhiv', qk_i, v_new))
        g_last = g_i[..., -1:]
        S = (jnp.exp(g_last)[..., None] * S
             + jnp.einsum('bhck,bhcv->bhkv', k_i * jnp.exp(g_last - g_i)[..., None], v_new))
        return S, o

    S, o = lax.scan(step, s0.astype(F32), (qc, kc, u_all, w_all, qk, gc))
    return _from_chunks(o), S


def gla_chunk_scan(q, k, v, g, s0):
    C = GLA_CHUNK
    qc, kc, vc = _to_chunks(q, C), _to_chunks(k, C), _to_chunks(v, C)
    gc = jnp.cumsum(_to_chunks(g, C), axis=-2)
    idx = jnp.arange(C)
    causal = (idx[:, None] >= idx[None, :])[..., None]

    def step(S, inp):
        q_i, k_i, v_i, b_i = inp
        b_last = b_i[..., -1:, :]
        o_inter = jnp.einsum('bhck,bhkv->bhcv', q_i * jnp.exp(b_i), S)
        dec = jnp.exp(jnp.where(causal, b_i[..., :, None, :] - b_i[..., None, :, :], -jnp.inf))
        att = jnp.einsum('bhik,bhjk,bhijk->bhij', q_i, k_i, dec)
        o = o_inter + jnp.einsum('bhij,bhjv->bhiv', att, v_i)
        S = (jnp.swapaxes(jnp.exp(b_last), -1, -2) * S
             + jnp.einsum('bhck,bhcv->bhkv', k_i * jnp.exp(b_last - b_i), v_i))
        return S, o

    S, o = lax.scan(step, s0.astype(F32), (qc, kc, vc, gc))
    return _from_chunks(o), S


def bidirectional(scan_fn, fwd_inputs, bwd_inputs, s0):
    o_f, s_f = scan_fn(*fwd_inputs, s0[:, 0])
    o_b, s_b = scan_fn(*[jnp.flip(t, 1) for t in bwd_inputs], s0[:, 1])
    return o_f + jnp.flip(o_b, 1), jnp.stack([s_f, s_b], axis=1)


def blocked_attention(q_nope, q_rope, k_nope, k_rope, v):
    B, T, H, _ = q_nope.shape
    nb = T // Q_BLOCK
    scale = (DN_B + DR_B) ** -0.5

    def blocks(x):
        return jnp.moveaxis(x.reshape((B, nb, Q_BLOCK) + x.shape[2:]), 1, 0)

    def attend(qs):
        qn, qr = qs
        s = jnp.einsum('bqhd,bkhd->bhqk', qn, k_nope) + jnp.einsum('bqhr,bkr->bhqk', qr, k_rope)
        p = jax.nn.softmax(s.astype(F32) * scale, axis=-1).astype(v.dtype)
        return jnp.einsum('bhqk,bkhd->bqhd', p, v)

    o = lax.map(attend, (blocks(q_nope), blocks(q_rope)))
    return jnp.moveaxis(o, 0, 1).reshape(B, T, H, v.shape[-1])


def even_mixer(h, w_in, conv_w, a_log, dt_bias, gdn_norm, q_norm, w_uq, kv_norm, w_ukv, w_out,
               gdn_s0, rope, ctx_kv):
    B, T, _ = h.shape
    qkv, z, a, b, cq, ckv, k_rope = _split(h @ w_in, EV_SIZES)
    qkv = jax.nn.silu(short_conv(qkv, conv_w))
    q, k, v = _split(qkv, (H_A * DK_A, H_A * DK_A, H_A * DV_A))
    q = l2norm(q.reshape(B, T, H_A, DK_A)) * (DK_A ** -0.5)
    k = l2norm(k.reshape(B, T, H_A, DK_A))
    v = v.reshape(B, T, H_A, DV_A)
    a = a.reshape(B, T, 2, H_A).astype(F32)
    b = b.reshape(B, T, 2, H_A).astype(F32)
    g = -jnp.exp(a_log.astype(F32)) * jax.nn.softplus(a + dt_bias.astype(F32))
    beta = jax.nn.sigmoid(b)
    o_a, s_fin = bidirectional(gdn_chunk_scan, (q, k, v, g[:, :, 0], beta[:, :, 0]),
                               (q, k, v, g[:, :, 1], beta[:, :, 1]), gdn_s0)
    o_a = (rmsnorm(o_a.astype(h.dtype), gdn_norm)
           * jax.nn.silu(z.reshape(B, T, H_A, DV_A))).reshape(B, T, H_A * DV_A)
    qh = (rmsnorm(cq, q_norm) @ w_uq).reshape(B, T, H_B, DN_B + DR_B)
    q_nope, q_rope = qh[..., :DN_B], qh[..., DN_B:]
    ckv = rmsnorm(ckv, kv_norm)
    k_rope = k_rope[:, :, None, :]
    if rope is not None:
        q_rope = rope_2d(q_rope, rope[0], rope[1])
        k_rope = rope_2d(k_rope, rope[0], rope[1])
    k_rope = k_rope[:, :, 0]
    ckv_all, krope_all = ckv, k_rope
    if ctx_kv is not None:
        ckv_all = jnp.concatenate([ctx_kv[0].astype(h.dtype), ckv], axis=1)
        krope_all = jnp.concatenate([ctx_kv[1].astype(h.dtype), k_rope], axis=1)
    S = ckv_all.shape[1]
    kv = (ckv_all @ w_ukv).reshape(B, S, H_B, DN_B + DV_B)
    o_b = blocked_attention(q_nope, q_rope, kv[..., :DN_B], krope_all, kv[..., DN_B:])
    o_b = o_b.reshape(B, T, H_B * DV_B)
    y = jnp.concatenate([o_a, o_b], axis=-1) @ w_out
    return y, s_fin, ckv, k_rope


def odd_mixer(h, w_in, w_gup, b_g, gla_norm, w_out, gla_s0):
    B, T, _ = h.shape
    q, k, v, r, gdown = _split(h @ w_in, OD_SIZES)
    q = q.reshape(B, T, H_C, DK_C) * (DK_C ** -0.5)
    k = k.reshape(B, T, H_C, DK_C)
    v = v.reshape(B, T, H_C, DV_C)
    gdown = gdown.reshape(B, T, 2, GATE_RANK).astype(F32)
    glog = jax.nn.log_sigmoid(jnp.einsum('btdr,drk->btdk', gdown, w_gup.astype(F32))
                              + b_g.astype(F32)) / GATE_NORM
    glog = glog.reshape(B, T, 2, H_C, DK_C)
    o, s_fin = bidirectional(gla_chunk_scan, (q, k, v, glog[:, :, 0]), (q, k, v, glog[:, :, 1]), gla_s0)
    o = (rmsnorm(o.astype(h.dtype), gla_norm)
         * jax.nn.silu(r.reshape(B, T, H_C, DV_C))).reshape(B, T, H_C * DV_C)
    return o @ w_out, s_fin


def setup_inputs(seed: int = 0) -> dict:
    ks = iter(jax.random.split(jax.random.key(seed), 32))

    def nrm(shape, s):
        return jax.random.normal(next(ks), shape, F32) * s

    def gain(shape):
        return 1.0 + 0.02 * jax.random.normal(next(ks), shape, F32)

    a_log = jnp.log(jax.random.uniform(next(ks), (N_EVEN, 2, H_A), F32, 1.0, 16.0))
    dt = jnp.exp(jax.random.uniform(next(ks), (N_EVEN, 2, H_A), F32, math.log(1e-3), math.log(0.1)))
    dt_bias = dt + jnp.log(-jnp.expm1(-dt))
    return {
        'x_prompt': nrm((BATCH, SEQ, D_MODEL), 1.0),
        'x_sample': nrm((DEC_BATCH, DEC_SEQ, D_MODEL), 1.0),
        'cache_mla_ckv': nrm((DEC_BATCH, N_EVEN, PAST_LEN, KV_LORA), 1.0),
        'cache_mla_krope': nrm((DEC_BATCH, N_EVEN, PAST_LEN, DR_B), 1.0),
        'state_gdn': nrm((DEC_BATCH, N_EVEN, 2, H_A, DK_A, DV_A), 0.5),
        'state_gla': nrm((DEC_BATCH, N_ODD, 2, H_C, DK_C, DV_C), 1.0),
        'c': nrm((DEC_BATCH, D_MODEL), 1.0),
        'c_ctx': nrm((D_MODEL,), 1.0),
        'mod_w': nrm((DEPTH, D_MODEL, N_MOD * D_MODEL), D_MODEL ** -0.5),
        'mod_b': nrm((DEPTH, N_MOD * D_MODEL), 0.02),
        'norm_pre': gain((DEPTH, 3, D_MODEL)),
        'norm_post': gain((DEPTH, 3, D_MODEL)),
        'ffn_w_in': nrm((DEPTH, 2, D_MODEL, 2 * D_FF), D_MODEL ** -0.5),
        'ffn_w_out': nrm((DEPTH, 2, D_FF, D_MODEL), D_FF ** -0.5),
        'ev_w_in': nrm((N_EVEN, D_MODEL, EV_IN), D_MODEL ** -0.5),
        'ev_conv_w': nrm((N_EVEN, CONV_W, 2 * H_A * DK_A + H_A * DV_A), CONV_W ** -0.5),
        'ev_gdn_a_log': a_log,
        'ev_gdn_dt_bias': dt_bias,
        'ev_gdn_norm': gain((N_EVEN, DV_A)),
        'ev_mla_q_norm': gain((N_EVEN, Q_LORA)),
        'ev_mla_w_uq': nrm((N_EVEN, Q_LORA, H_B * (DN_B + DR_B)), Q_LORA ** -0.5),
        'ev_mla_kv_norm': gain((N_EVEN, KV_LORA)),
        'ev_mla_w_ukv': nrm((N_EVEN, KV_LORA, H_B * (DN_B + DV_B)), KV_LORA ** -0.5),
        'ev_w_out': nrm((N_EVEN, EV_MIX, D_MODEL), EV_MIX ** -0.5),
        'od_w_in': nrm((N_ODD, D_MODEL, OD_IN), D_MODEL ** -0.5),
        'od_gla_w_gup': nrm((N_ODD, 2, GATE_RANK, H_C * DK_C), GATE_RANK ** -0.5),
        'od_gla_b_g': nrm((N_ODD, 2, H_C * DK_C), 0.1),
        'od_gla_norm': gain((N_ODD, DV_C)),
        'od_w_out': nrm((N_ODD, OD_MIX, D_MODEL), OD_MIX ** -0.5),
    }


def reference(x_prompt, x_sample, cache_mla_ckv, cache_mla_krope, state_gdn, state_gla, c, c_ctx,
              mod_w, mod_b, norm_pre, norm_post, ffn_w_in, ffn_w_out,
              ev_w_in, ev_conv_w, ev_gdn_a_log, ev_gdn_dt_bias, ev_gdn_norm,
              ev_mla_q_norm, ev_mla_w_uq, ev_mla_kv_norm, ev_mla_w_ukv, ev_w_out,
              od_w_in, od_gla_w_gup, od_gla_b_g, od_gla_norm, od_w_out):
    cos, sin = grid_rope(x_sample.shape[1])
    xc, xs = x_prompt, x_sample
    new_ckv, new_krope, new_gdn, new_gla = [], [], [], []
    for i in range(DEPTH):
        m_ctx = adaln(c_ctx[None, :], mod_w[i], mod_b[i])
        m_lat = adaln(c, mod_w[i], mod_b[i])
        xc = ffn_branch(xc, m_ctx[0:3], norm_pre[i, 0], norm_post[i, 0], ffn_w_in[i, 0], ffn_w_out[i, 0])
        xs = ffn_branch(xs, m_lat[0:3], norm_pre[i, 0], norm_post[i, 0], ffn_w_in[i, 0], ffn_w_out[i, 0])
        hc = modulate(xc, norm_pre[i, 1], m_ctx[3], m_ctx[4])
        hs = modulate(xs, norm_pre[i, 1], m_lat[3], m_lat[4])
        if i % 2 == 0:
            e = i // 2
            w = (ev_w_in[e], ev_conv_w[e], ev_gdn_a_log[e], ev_gdn_dt_bias[e], ev_gdn_norm[e],
                 ev_mla_q_norm[e], ev_mla_w_uq[e], ev_mla_kv_norm[e], ev_mla_w_ukv[e], ev_w_out[e])
            s0 = jnp.zeros((xc.shape[0], 2, H_A, DK_A, DV_A), F32)
            yc, s_ctx, ckv_ctx, krope_ctx = even_mixer(hc, *w, s0, None, None)
            ys = even_mixer(hs, *w, state_gdn[:, e], (cos, sin),
                            (cache_mla_ckv[:, e], cache_mla_krope[:, e]))[0]
            new_ckv.append(ckv_ctx)
            new_krope.append(krope_ctx)
            new_gdn.append(s_ctx.astype(xc.dtype))
        else:
            o = i // 2
            w = (od_w_in[o], od_gla_w_gup[o], od_gla_b_g[o], od_gla_norm[o], od_w_out[o])
            s0 = jnp.zeros((xc.shape[0], 2, H_C, DK_C, DV_C), F32)
            yc, s_ctx = odd_mixer(hc, *w, s0)
            ys = odd_mixer(hs, *w, state_gla[:, o])[0]
            new_gla.append(s_ctx.astype(xc.dtype))
        xc = xc + m_ctx[5] * rmsnorm(yc, norm_post[i, 1])
        xs = xs + m_lat[5] * rmsnorm(ys, norm_post[i, 1])
        xc = ffn_branch(xc, m_ctx[6:9], norm_pre[i, 2], norm_post[i, 2], ffn_w_in[i, 1], ffn_w_out[i, 1])
        xs = ffn_branch(xs, m_lat[6:9], norm_pre[i, 2], norm_post[i, 2], ffn_w_in[i, 1], ffn_w_out[i, 1])
    return (xc, xs, jnp.stack(new_ckv, axis=1), jnp.stack(new_krope, axis=1),
            jnp.stack(new_gdn, axis=1), jnp.stack(new_gla, axis=1))
```

```python
import functools
import math

import jax
import jax.numpy as jnp
from jax import lax
from jax.experimental import pallas as pl
from jax.experimental.pallas import tpu as pltpu

F32 = jnp.float32
BF16 = jnp.bfloat16

D_MODEL = 1024
GRID_W = 64
N_MOD = 9
D_FF = 2816
FFN_RES = 0.5
EPS = 1e-6
H_A, DK_A, DV_A = 4, 128, 128
CONV_W = 5
H_B, Q_LORA, KV_LORA, DN_B, DR_B, DV_B = 4, 256, 128, 128, 64, 128
ROPE_BASE = 10000.0
H_C, DK_C, DV_C = 4, 128, 256
GATE_RANK = 16
GATE_NORM = 16.0

LANES = 128
SUBLANES = 8
VMEM_LIMIT = 56 * 1024 * 1024
GDN_C = 64
GLA_C = 64
GLA_SUB = 16
QKV_A = 2 * H_A * DK_A + H_A * DV_A
EV_COLS = QKV_A + H_A * DV_A + Q_LORA + KV_LORA + 3 * LANES
OD_COLS = 2 * H_C * DK_C + 2 * H_C * DV_C + LANES


def _cparams(sem):
    return pltpu.CompilerParams(dimension_semantics=sem, vmem_limit_bytes=VMEM_LIMIT)


def _sigmoid(x):
    return 1.0 / (1.0 + jnp.exp(-x))


def _silu(x):
    return x * _sigmoid(x)


def _softplus(x):
    return jnp.maximum(x, 0.0) + jnp.log1p(jnp.exp(-jnp.abs(x)))


def _rms(x, w):
    return x * lax.rsqrt(jnp.mean(x * x, axis=-1, keepdims=True) + EPS) * w


def _dot(a, b):
    return jnp.dot(a.astype(BF16), b.astype(BF16), preferred_element_type=F32)


def _dot_nt(a, b):
    return lax.dot_general(a.astype(BF16), b.astype(BF16), (((1,), (1,)), ((), ())),
                           preferred_element_type=F32)


def _split3(x):
    hi = x.astype(BF16)
    r = x - hi.astype(F32)
    mid = r.astype(BF16)
    lo = (r - mid.astype(F32)).astype(BF16)
    return hi, mid, lo


def _dot3(a, b):
    a_hi, a_lo, _ = _split3(a)
    b_hi, b_lo, _ = _split3(b)
    d = lambda p, q: jnp.dot(p, q, preferred_element_type=F32)
    return d(a_hi, b_hi) + (d(a_hi, b_lo) + d(a_lo, b_hi))


def _adaln_kernel(c_ref, w_ref, b_ref, o_ref):
    o_ref[0] = _dot(_silu(c_ref[...]), w_ref[0]) + b_ref[0]


def _adaln(cond8, mod_w, mod_b):
    depth, d, n = mod_w.shape
    tn = 1152
    return pl.pallas_call(
        _adaln_kernel,
        grid=(depth, n // tn),
        in_specs=[pl.BlockSpec((SUBLANES, d), lambda l, j: (0, 0)),
                  pl.BlockSpec((1, d, tn), lambda l, j: (l, 0, j)),
                  pl.BlockSpec((1, 1, tn), lambda l, j: (l, 0, j))],
        out_specs=pl.BlockSpec((1, SUBLANES, tn), lambda l, j: (l, 0, j)),
        out_shape=jax.ShapeDtypeStruct((depth, SUBLANES, n), F32),
        compiler_params=_cparams(("arbitrary", "arbitrary")),
        name="adaln",
    )(cond8, mod_w, mod_b.reshape(depth, 1, n))


class _Geo:
    def __init__(self, b_c, t_c, b_l, t_l):
        self.b_c, self.t_c, self.b_l, self.t_l = b_c, t_c, b_l, t_l
        self.n_c = b_c * t_c
        self.n = self.n_c + b_l * t_l
        assert self.n_c % t_l == 0 or b_l == 0

    def tile(self, cap):
        t = math.gcd(self.t_c * self.b_c, self.t_l)
        while t > cap:
            t //= 2
        return t

    def group(self, i, tm):
        nct = self.n_c // tm
        return jnp.where(i < nct, 0, 1 + (i - nct) // (self.t_l // tm))


def _mod_spec(geo, tm):
    return pl.BlockSpec((1, N_MOD, D_MODEL), lambda i: (geo.group(i, tm), 0, 0))


def _row_spec(width):
    return pl.BlockSpec((1, width), lambda i: (0, 0))


def _full_spec(shape):
    return pl.BlockSpec(shape, lambda i: (0,) * len(shape), pipeline_mode=pl.Buffered(1))


def _ffn_kernel(x_ref, mod_ref, gpre_ref, gpost_ref, win_ref, wout_ref, o_ref, acc_ref, *, j0, fc):
    x = x_ref[...]
    shift, scale, gate = mod_ref[0, j0:j0 + 1], mod_ref[0, j0 + 1:j0 + 2], mod_ref[0, j0 + 2:j0 + 3]
    h = (_rms(x, gpre_ref[...]) * (1.0 + scale) + shift).astype(BF16)
    for c in range(D_FF // fc):
        u = jnp.dot(h, win_ref[:, c * fc:(c + 1) * fc], preferred_element_type=F32)
        v = jnp.dot(h, win_ref[:, D_FF + c * fc:D_FF + (c + 1) * fc], preferred_element_type=F32)
        a = (_silu(u) * v).astype(BF16)
        part = jnp.dot(a, wout_ref[c * fc:(c + 1) * fc, :], preferred_element_type=F32)
        if c == 0:
            acc_ref[...] = part
        else:
            acc_ref[...] += part
    o_ref[...] = x + FFN_RES * gate * _rms(acc_ref[...], gpost_ref[...])


def _ffn(geo, x, mods, g_pre, g_post, w_in, w_out, j0):
    tm = geo.tile(512)
    kern = functools.partial(_ffn_kernel, j0=j0, fc=256)
    return pl.pallas_call(
        kern,
        grid=(geo.n // tm,),
        in_specs=[pl.BlockSpec((tm, D_MODEL), lambda i: (i, 0)),
                  _mod_spec(geo, tm), _row_spec(D_MODEL), _row_spec(D_MODEL),
                  _full_spec((D_MODEL, 2 * D_FF)), _full_spec((D_FF, D_MODEL))],
        out_specs=pl.BlockSpec((tm, D_MODEL), lambda i: (i, 0)),
        out_shape=jax.ShapeDtypeStruct((geo.n, D_MODEL), F32),
        scratch_shapes=[pltpu.VMEM((tm, D_MODEL), F32)],
        compiler_params=_cparams(("arbitrary",)),
        name="ffn",
    )(x, mods, g_pre.reshape(1, -1), g_post.reshape(1, -1), w_in, w_out)


def _evproj_kernel(x_ref, mod_ref, gpre_ref, w1_ref, qn_ref, wuq_ref, kvn_ref, wukv_ref, alog_ref,
                   dtb_ref, cos_ref, sin_ref,
                   qkv_ref, z_ref, gb_ref, q_ref, ckv_ref, kr_ref, k_ref, v_ref):
    x = x_ref[...]
    shift, scale = mod_ref[0, 3:4], mod_ref[0, 4:5]
    h = (_rms(x, gpre_ref[...]) * (1.0 + scale) + shift).astype(BF16)
    y = jnp.dot(h, w1_ref[...], preferred_element_type=F32)
    qkv_ref[...] = y[:, :QKV_A]
    o = QKV_A
    z_ref[...] = y[:, o:o + H_A * DV_A]
    o += H_A * DV_A
    cq = y[:, o:o + Q_LORA]
    o += Q_LORA
    ckv = y[:, o:o + KV_LORA]
    o += KV_LORA
    kr, krs, ab = y[:, o:o + LANES], y[:, o + LANES:o + 2 * LANES], y[:, o + 2 * LANES:o + 3 * LANES]
    lane = lax.broadcasted_iota(jnp.int32, ab.shape, 1)
    g = -jnp.exp(alog_ref[...]) * _softplus(ab + dtb_ref[...])
    gb_ref[...] = jnp.where(lane < 2 * H_A, g, jnp.where(lane < 4 * H_A, _sigmoid(ab), 0.0))
    cos, sin = cos_ref[...], sin_ref[...]
    sc = (DN_B + DR_B) ** -0.5
    qh = _dot(_rms(cq, qn_ref[...]), wuq_ref[...])
    for hh in range(H_B):
        b0 = hh * 2 * LANES
        q_ref[:, b0:b0 + LANES] = (qh[:, b0:b0 + LANES] * sc).astype(BF16)
        rp = qh[:, b0 + LANES:b0 + 2 * LANES]
        rs = qh[:, 2 * LANES * H_B + hh * LANES:2 * LANES * H_B + (hh + 1) * LANES]
        q_ref[:, b0 + LANES:b0 + 2 * LANES] = ((rp * cos + rs * sin) * sc).astype(BF16)
    ckvn = _rms(ckv, kvn_ref[...])
    ckv_ref[...] = ckvn
    kro = kr * cos + krs * sin
    kr_ref[...] = kro[:, :DR_B]
    kv = _dot(ckvn, wukv_ref[...])
    for hh in range(H_B):
        b0 = hh * 2 * LANES
        k_ref[:, b0:b0 + LANES] = kv[:, b0:b0 + LANES].astype(BF16)
        k_ref[:, b0 + LANES:b0 + 2 * LANES] = kro.astype(BF16)
        v_ref[:, hh * LANES:(hh + 1) * LANES] = kv[:, b0 + LANES:b0 + 2 * LANES].astype(BF16)


def _evproj(geo, x, mods, g_pre, w1, q_norm, wuq, kv_norm, wukv, alog, dtb, cos_t, sin_t, tm):
    n = geo.n
    nct = geo.n_c // tm
    tpl = geo.t_l // tm

    def rope_idx(i):
        return (jnp.where(i < nct, 0, 1 + (i - nct) % tpl), 0)

    tok = lambda w: pl.BlockSpec((tm, w), lambda i: (i, 0))
    outs = [(QKV_A, F32), (H_A * DV_A, F32), (LANES, F32), (2 * LANES * H_B, BF16), (KV_LORA, F32),
            (DR_B, F32), (2 * LANES * H_B, BF16), (H_B * DV_B, BF16)]
    return pl.pallas_call(
        _evproj_kernel,
        grid=(n // tm,),
        in_specs=[tok(D_MODEL), _mod_spec(geo, tm), _row_spec(D_MODEL), _full_spec(w1.shape),
                  _row_spec(Q_LORA), _full_spec(wuq.shape), _row_spec(KV_LORA), _full_spec(wukv.shape),
                  _row_spec(LANES), _row_spec(LANES),
                  pl.BlockSpec((tm, LANES), rope_idx), pl.BlockSpec((tm, LANES), rope_idx)],
        out_specs=[tok(w) for w, _ in outs],
        out_shape=[jax.ShapeDtypeStruct((n, w), dt) for w, dt in outs],
        compiler_params=_cparams(("arbitrary",)),
        name="even_proj",
    )(x, mods, g_pre.reshape(1, -1), w1, q_norm.reshape(1, -1), wuq, kv_norm.reshape(1, -1), wukv,
      alog, dtb, cos_t, sin_t)


def _cachekv_kernel(ckv_ref, kr_ref, wukv_ref, k_ref, v_ref):
    kv = _dot(ckv_ref[0], wukv_ref[...])
    kr = kr_ref[0].astype(BF16)
    for hh in range(H_B):
        b0 = hh * 2 * LANES
        k_ref[0, :, b0:b0 + LANES] = kv[:, b0:b0 + LANES].astype(BF16)
        k_ref[0, :, b0 + LANES:b0 + 2 * LANES] = kr
        v_ref[0, :, hh * LANES:(hh + 1) * LANES] = kv[:, b0 + LANES:b0 + 2 * LANES].astype(BF16)


def _cachekv(ckv, krope_pad, wukv):
    b, p, _ = ckv.shape
    return pl.pallas_call(
        _cachekv_kernel,
        grid=(b,),
        in_specs=[pl.BlockSpec((1, p, KV_LORA), lambda i: (i, 0, 0)),
                  pl.BlockSpec((1, p, LANES), lambda i: (i, 0, 0)),
                  _full_spec(wukv.shape)],
        out_specs=[pl.BlockSpec((1, p, 2 * LANES * H_B), lambda i: (i, 0, 0)),
                   pl.BlockSpec((1, p, H_B * DV_B), lambda i: (i, 0, 0))],
        out_shape=[jax.ShapeDtypeStruct((b, p, 2 * LANES * H_B), BF16),
                   jax.ShapeDtypeStruct((b, p, H_B * DV_B), BF16)],
        compiler_params=_cparams(("arbitrary",)),
        name="cache_kv",
    )(ckv, krope_pad, wukv)


def _conv_kernel(x_ref, prev_ref, next_ref, w_ref, o_ref, xp_ref, *, tc, geo):
    i = pl.program_id(0)
    t0 = i * tc
    in_ctx = t0 < geo.n_c
    pos = jnp.where(in_ctx, t0 % geo.t_c, (t0 - geo.n_c) % geo.t_l)
    seq_len = jnp.where(in_ctx, geo.t_c, geo.t_l)
    has_prev = pos > 0
    has_next = pos + tc < seq_len
    half = CONV_W // 2
    for cg in range(QKV_A // LANES):
        cs = slice(cg * LANES, (cg + 1) * LANES)
        xp_ref[0:SUBLANES, :] = jnp.where(has_prev, prev_ref[:, cs], 0.0)
        xp_ref[SUBLANES:SUBLANES + tc, :] = x_ref[:, cs]
        xp_ref[SUBLANES + tc:2 * SUBLANES + tc, :] = jnp.where(has_next, next_ref[:, cs], 0.0)
        acc = None
        for k in range(CONV_W):
            r0 = SUBLANES - half + k
            term = w_ref[k:k + 1, cs] * xp_ref[r0:r0 + tc, :]
            acc = term if acc is None else acc + term
        y = _silu(acc)
        if cg < 2 * H_A:
            y = y * lax.rsqrt(jnp.sum(y * y, axis=-1, keepdims=True) + EPS)
            if cg < H_A:
                y = y * (DK_A ** -0.5)
        o_ref[:, cs] = y


def _conv(geo, qkv_pre, conv_w):
    tc = min(math.gcd(geo.t_c, geo.t_l), 256)
    n = geo.n
    nb8 = n // SUBLANES
    r = tc // SUBLANES
    kern = functools.partial(_conv_kernel, tc=tc, geo=geo)
    return pl.pallas_call(
        kern,
        grid=(n // tc,),
        in_specs=[pl.BlockSpec((tc, QKV_A), lambda i: (i, 0)),
                  pl.BlockSpec((SUBLANES, QKV_A), lambda i: (jnp.maximum(i * r - 1, 0), 0)),
                  pl.BlockSpec((SUBLANES, QKV_A), lambda i: (jnp.minimum((i + 1) * r, nb8 - 1), 0)),
                  _full_spec((CONV_W, QKV_A))],
        out_specs=pl.BlockSpec((tc, QKV_A), lambda i: (i, 0)),
        out_shape=jax.ShapeDtypeStruct((n, QKV_A), F32),
        scratch_shapes=[pltpu.VMEM((tc + 2 * SUBLANES, LANES), F32)],
        compiler_params=_cparams(("arbitrary",)),
        name="gdn_conv",
    )(qkv_pre, qkv_pre, qkv_pre, conv_w)


def _gdn_direction(qkv, gb, s_all, d, reverse):
    c, h = GDN_C, H_A
    r = h * c
    sh = c.bit_length() - 1
    stack = lambda off: jnp.concatenate([qkv[:, off + i * DK_A:off + (i + 1) * DK_A] for i in range(h)], axis=0)
    qs, ks, vs = stack(0), stack(h * DK_A), stack(2 * h * DK_A)
    col = lambda off: jnp.concatenate([gb[:, off + i:off + i + 1] for i in range(h)], axis=0)
    g_col, beta_col = col(d * h), col(2 * h + d * h)
    ii = lax.broadcasted_iota(jnp.int32, (r, r), 0)
    jj = lax.broadcasted_iota(jnp.int32, (r, r), 1)
    same = (ii >> sh) == (jj >> sh)
    later = (ii <= jj) if reverse else (ii >= jj)
    earlier = (ii >= jj) if reverse else (ii <= jj)
    mask = lambda m, x: jnp.where(same, jnp.where(m, x, 0.0), 0.0)
    g_row = jnp.sum(jnp.where(ii == jj, g_col, 0.0), axis=0, keepdims=True)
    gc_col = jnp.sum(mask(later, g_row), axis=1, keepdims=True)
    gc_row = jnp.sum(mask(earlier, g_col), axis=0, keepdims=True)
    gl_col = jnp.sum(jnp.where(same, g_row, 0.0), axis=1, keepdims=True)
    decay = mask(later, jnp.exp(jnp.minimum(gc_col - gc_row, 0.0)))
    kb = ks * beta_col
    a_mat = jnp.where(ii == jj, 0.0, _dot_nt(kb, ks) * decay)
    t_inv = jnp.where(ii == jj, 1.0, 0.0) - jnp.where((ii >> 1) == (jj >> 1), a_mat, 0.0)
    for lv in range(1, sh):
        e = jnp.where((ii >> (lv + 1)) == (jj >> (lv + 1)),
                      jnp.where((ii >> lv) == (jj >> lv), 0.0, a_mat), 0.0)
        t_inv = t_inv - _dot(_dot(t_inv, e), t_inv)
    egc = jnp.exp(gc_col)
    sol = _dot3(t_inv, jnp.concatenate([vs * beta_col, kb * egc], axis=1))
    u, w = sol[:, :DV_A], sol[:, DV_A:]
    qk = _dot_nt(qs, ks) * decay
    diag = lambda x: jnp.concatenate([x[i * c:(i + 1) * c, i * DV_A:(i + 1) * DV_A] for i in range(h)], axis=0)
    v_new = u - diag(_dot(w, s_all))
    o = diag(_dot(qs * egc, s_all)) + _dot(qk, v_new)
    kd_t = (ks * jnp.exp(gl_col - gc_col)).T
    row_head = lax.broadcasted_iota(jnp.int32, (r, DV_A), 0) >> sh
    v_exp = jnp.concatenate([jnp.where(row_head == i, v_new, 0.0) for i in range(h)], axis=1)
    gl_row = jnp.concatenate(
        [jnp.broadcast_to(jnp.sum(g_col[i * c:(i + 1) * c], axis=0, keepdims=True), (1, DV_A)) for i in range(h)],
        axis=1)
    s_new = s_all * jnp.exp(gl_row) + _dot(kd_t, v_exp)
    o_out = jnp.concatenate([o[i * c:(i + 1) * c] for i in range(h)], axis=1)
    return o_out, s_new


def _gdn_kernel(*refs, has_s0):
    if has_s0:
        qf_ref, qb_ref, gf_ref, gbk_ref, s0_ref, _, _, of_ref, ob_ref, sfin_ref, s_scr = refs
    else:
        qf_ref, qb_ref, gf_ref, gbk_ref, of_ref, ob_ref, sfin_ref, s_scr = refs
    n = pl.program_id(1)

    @pl.when(n == 0)
    def _():
        for d in range(2):
            for i in range(H_A):
                s_scr[d, :, i * DV_A:(i + 1) * DV_A] = (
                    s0_ref[0, d, i] if has_s0 else jnp.zeros((DK_A, DV_A), F32))

    o_f, s_f = _gdn_direction(qf_ref[...], gf_ref[...], s_scr[0], 0, False)
    o_b, s_b = _gdn_direction(qb_ref[...], gbk_ref[...], s_scr[1], 1, True)
    of_ref[...] = o_f
    ob_ref[...] = o_b
    s_scr[0] = s_f
    s_scr[1] = s_b

    @pl.when(n == pl.num_programs(1) - 1)
    def _():
        for d, s in ((0, s_f), (1, s_b)):
            for i in range(H_A):
                sfin_ref[0, d, i] = s[:, i * DV_A:(i + 1) * DV_A]


def _gdn(qkv, gb, s0, prev, nb, t, tok0):
    c = GDN_C
    nc = t // c
    b0 = tok0 // c
    fwd = lambda b, n: (b0 + b * nc + n, 0)
    bwd = lambda b, n: (b0 + b * nc + nc - 1 - n, 0)
    st_spec = pl.BlockSpec((1, 2, H_A, DK_A, DV_A), lambda b, n: (b, 0, 0, 0, 0))
    in_specs = [pl.BlockSpec((c, QKV_A), fwd), pl.BlockSpec((c, QKV_A), bwd),
                pl.BlockSpec((c, LANES), fwd), pl.BlockSpec((c, LANES), bwd)]
    args = [qkv, qkv, gb, gb]
    aliases = {}
    if s0 is not None:
        in_specs += [st_spec, pl.BlockSpec(memory_space=pl.ANY), pl.BlockSpec(memory_space=pl.ANY)]
        args += [s0, prev[0], prev[1]]
        aliases = {5: 0, 6: 1}
    ntok = qkv.shape[0]
    return pl.pallas_call(
        functools.partial(_gdn_kernel, has_s0=s0 is not None),
        grid=(nb, nc),
        in_specs=in_specs,
        out_specs=[pl.BlockSpec((c, H_A * DV_A), fwd), pl.BlockSpec((c, H_A * DV_A), bwd), st_spec],
        out_shape=[jax.ShapeDtypeStruct((ntok, H_A * DV_A), F32),
                   jax.ShapeDtypeStruct((ntok, H_A * DV_A), F32),
                   jax.ShapeDtypeStruct((nb, 2, H_A, DK_A, DV_A), F32)],
        scratch_shapes=[pltpu.VMEM((2, DK_A, H_A * DV_A), F32)],
        input_output_aliases=aliases,
        compiler_params=_cparams(("arbitrary", "arbitrary")),
        name="gdn_scan",
    )(*args)


def _attn_kernel(*refs, kb, has_cache):
    if has_cache:
        q_ref, k_ref, v_ref, kc_ref, vc_ref, o_ref = refs
    else:
        q_ref, k_ref, v_ref, o_ref = refs
    s_len = k_ref.shape[0]
    for hh in range(H_B):
        ks_ = slice(hh * 2 * LANES, (hh + 1) * 2 * LANES)
        vs_ = slice(hh * DV_B, (hh + 1) * DV_B)
        q = q_ref[:, ks_]
        blocks = [(k_ref[j * kb:(j + 1) * kb, ks_], v_ref[j * kb:(j + 1) * kb, vs_]) for j in range(s_len // kb)]
        if has_cache:
            blocks.append((kc_ref[0, :, ks_], vc_ref[0, :, vs_]))
        m = l = acc = None
        for kblk, vblk in blocks:
            s = lax.dot_general(q, kblk, (((1,), (1,)), ((), ())), preferred_element_type=F32)
            bm = jnp.max(s, axis=-1, keepdims=True)
            if m is None:
                m = bm
                p = jnp.exp(s - m)
                l = jnp.sum(p, axis=-1, keepdims=True)
                acc = jnp.dot(p.astype(BF16), vblk, preferred_element_type=F32)
            else:
                m_new = jnp.maximum(m, bm)
                alpha = jnp.exp(m - m_new)
                p = jnp.exp(s - m_new)
                l = alpha * l + jnp.sum(p, axis=-1, keepdims=True)
                acc = alpha * acc + jnp.dot(p.astype(BF16), vblk, preferred_element_type=F32)
                m = m_new
        o_ref[:, vs_] = (acc / l).astype(o_ref.dtype)


def _attn_ctx(geo, q, k, v):
    t = geo.t_c
    kw, vw = 2 * LANES * H_B, H_B * DV_B
    return pl.pallas_call(
        functools.partial(_attn_kernel, kb=min(t, 512), has_cache=False),
        grid=(geo.b_c,),
        in_specs=[pl.BlockSpec((t, kw), lambda b: (b, 0)), pl.BlockSpec((t, kw), lambda b: (b, 0)),
                  pl.BlockSpec((t, vw), lambda b: (b, 0))],
        out_specs=pl.BlockSpec((t, vw), lambda b: (b, 0)),
        out_shape=jax.ShapeDtypeStruct((geo.n, vw), BF16),
        compiler_params=_cparams(("arbitrary",)),
        name="mla_ctx",
    )(q, k, v)


def _attn_lat(geo, q, k, v, kc, vc, o_prev):
    t = geo.t_l
    tq = min(t, 256)
    kw, vw = 2 * LANES * H_B, H_B * DV_B
    q0 = geo.n_c // tq
    s0 = geo.n_c // t
    p = kc.shape[1]
    qmap = lambda b, i: (q0 + b * (t // tq) + i, 0)
    kern = functools.partial(_attn_kernel, kb=min(t, 512), has_cache=True)

    def body(q_ref, k_ref, v_ref, kc_ref, vc_ref, prev_ref, o_ref):
        del prev_ref
        kern(q_ref, k_ref, v_ref, kc_ref, vc_ref, o_ref)

    return pl.pallas_call(
        body,
        grid=(geo.b_l, t // tq),
        in_specs=[pl.BlockSpec((tq, kw), qmap),
                  pl.BlockSpec((t, kw), lambda b, i: (s0 + b, 0)),
                  pl.BlockSpec((t, vw), lambda b, i: (s0 + b, 0)),
                  pl.BlockSpec((1, p, kw), lambda b, i: (b, 0, 0)),
                  pl.BlockSpec((1, p, vw), lambda b, i: (b, 0, 0)),
                  pl.BlockSpec(memory_space=pl.ANY)],
        out_specs=pl.BlockSpec((tq, vw), qmap),
        out_shape=jax.ShapeDtypeStruct((geo.n, vw), BF16),
        input_output_aliases={5: 0},
        compiler_params=_cparams(("arbitrary", "arbitrary")),
        name="mla_lat",
    )(q, k, v, kc, vc, o_prev)


def _mixout_kernel(*refs, heads, dv, n_extra):
    if n_extra:
        of_ref, ob_ref, gate_ref, extra_ref, x_ref, mod_ref, gn_ref, gpost_ref, wout_ref, o_ref = refs
    else:
        of_ref, ob_ref, gate_ref, x_ref, mod_ref, gn_ref, gpost_ref, wout_ref, o_ref = refs
    parts = []
    for i in range(heads):
        cs = slice(i * dv, (i + 1) * dv)
        o = of_ref[:, cs] + ob_ref[:, cs]
        parts.append((_rms(o, gn_ref[...]) * _silu(gate_ref[:, cs])).astype(BF16))
    if n_extra:
        parts.append(extra_ref[...])
    y = jnp.dot(jnp.concatenate(parts, axis=1), wout_ref[...], preferred_element_type=F32)
    o_ref[...] = x_ref[...] + mod_ref[0, 5:6] * _rms(y, gpost_ref[...])


def _mixout(geo, o_f, o_b, gate, extra, x, mods, g_norm, g_post, w_out, heads, dv):
    tm = geo.tile(512)
    tok = lambda w: pl.BlockSpec((tm, w), lambda i: (i, 0))
    wd = heads * dv
    in_specs = [tok(wd), tok(wd), tok(wd)]
    args = [o_f, o_b, gate]
    if extra is not None:
        in_specs.append(tok(extra.shape[1]))
        args.append(extra)
    in_specs += [tok(D_MODEL), _mod_spec(geo, tm), _row_spec(dv), _row_spec(D_MODEL), _full_spec(w_out.shape)]
    args += [x, mods, g_norm.reshape(1, -1), g_post.reshape(1, -1), w_out]
    return pl.pallas_call(
        functools.partial(_mixout_kernel, heads=heads, dv=dv, n_extra=extra is not None),
        grid=(geo.n // tm,),
        in_specs=in_specs,
        out_specs=tok(D_MODEL),
        out_shape=jax.ShapeDtypeStruct((geo.n, D_MODEL), F32),
        compiler_params=_cparams(("arbitrary",)),
        name="mixer_out",
    )(*args)


def _odproj_kernel(x_ref, mod_ref, gpre_ref, w1_ref, wg_ref, bg_ref, q_ref, k_ref, v_ref, r_ref, gl_ref):
    x = x_ref[...]
    shift, scale = mod_ref[0, 3:4], mod_ref[0, 4:5]
    h = (_rms(x, gpre_ref[...]) * (1.0 + scale) + shift).astype(BF16)
    y = jnp.dot(h, w1_ref[...], preferred_element_type=F32)
    kq = H_C * DK_C
    vv = H_C * DV_C
    q_ref[...] = y[:, :kq] * (DK_C ** -0.5)
    k_ref[...] = y[:, kq:2 * kq]
    v_ref[...] = y[:, 2 * kq:2 * kq + vv]
    r_ref[...] = y[:, 2 * kq + vv:2 * kq + 2 * vv]
    gd = y[:, 2 * kq + 2 * vv:]
    pre = _dot(gd, wg_ref[...]) + bg_ref[...]
    gl_ref[...] = -_softplus(-pre) * (1.0 / GATE_NORM)


def _odproj(geo, x, mods, g_pre, w1, wg, bg):
    tm = geo.tile(512)
    n = geo.n
    tok = lambda w: pl.BlockSpec((tm, w), lambda i: (i, 0))
    widths = [H_C * DK_C, H_C * DK_C, H_C * DV_C, H_C * DV_C, 2 * H_C * DK_C]
    return pl.pallas_call(
        _odproj_kernel,
        grid=(n // tm,),
        in_specs=[tok(D_MODEL), _mod_spec(geo, tm), _row_spec(D_MODEL), _full_spec(w1.shape),
                  _full_spec(wg.shape), _row_spec(2 * H_C * DK_C)],
        out_specs=[tok(w) for w in widths],
        out_shape=[jax.ShapeDtypeStruct((n, w), F32) for w in widths],
        compiler_params=_cparams(("arbitrary",)),
        name="odd_proj",
    )(x, mods, g_pre.reshape(1, -1), w1, wg, bg)


def _gla_direction(q_ref, k_ref, v_ref, g_ref, st_ref, o_ref, b_scr, d, reverse):
    c, sb = GLA_C, GLA_SUB
    nsb = c // sb
    ii = lax.broadcasted_iota(jnp.int32, (c, c), 0)
    jj = lax.broadcasted_iota(jnp.int32, (c, c), 1)
    tri = jnp.where((ii <= jj) if reverse else (ii >= jj), 1.0, 0.0)
    b_all = None
    for piece in _split3(g_ref[...]):
        term = jnp.dot(tri.astype(BF16), piece, preferred_element_type=F32)
        b_all = term if b_all is None else b_all + term
    b_scr[...] = b_all
    last = 0 if reverse else c - 1
    row_i = lax.broadcasted_iota(jnp.int32, (sb, LANES), 0)
    lane_i = lax.broadcasted_iota(jnp.int32, (sb, LANES), 1)
    col_c = lax.broadcasted_iota(jnp.int32, (sb, c), 1)
    for h in range(H_C):
        ks_ = slice(h * DK_C, (h + 1) * DK_C)
        q, k, v = q_ref[:, ks_], k_ref[:, ks_], v_ref[:, h * DV_C:(h + 1) * DV_C]
        b = b_scr[:, ks_]
        b_last = b_scr[last:last + 1, ks_]
        st = st_ref[d, h]
        o_inter = _dot_nt(q * jnp.exp(b), st)
        att_rows = []
        for blk in range(nsb):
            r0 = (c - sb * (blk + 1)) if reverse else sb * blk
            first = (r0 + sb - 1) if reverse else r0
            q_b, b_b = q[r0:r0 + sb], b[r0:r0 + sb]
            dmat = jnp.zeros((sb, LANES), F32)
            for j in range(sb):
                rj = r0 + j
                e = jnp.exp(jnp.minimum(b_b - b_scr[rj:rj + 1, ks_], 0.0))
                col = jnp.sum(q_b * e * k_ref[rj:rj + 1, ks_], axis=-1, keepdims=True)
                dmat = jnp.where(lane_i == rj, col, dmat)
            valid = (row_i + r0 <= lane_i) if reverse else (row_i + r0 >= lane_i)
            rows = jnp.where(valid, dmat, 0.0)[:, :c]
            if blk > 0:
                bm = b_scr[first:first + 1, ks_]
                qt = q_b * jnp.exp(b_b - bm)
                kt = k * jnp.exp(jnp.minimum(bm - b, 0.0))
                off = _dot_nt(qt, kt)
                before = (col_c >= r0 + sb) if reverse else (col_c < r0)
                rows = rows + jnp.where(before, off, 0.0)
            att_rows.append(rows)
        if reverse:
            att_rows = att_rows[::-1]
        att = jnp.concatenate(att_rows, axis=0)
        o_ref[:, h * DV_C:(h + 1) * DV_C] = o_inter + _dot(att, v)
        kd = k * jnp.exp(b_last - b)
        st_ref[d, h] = st * jnp.exp(b_last) + _dot(v.T, kd)


def _gla_kernel(*refs, has_s0):
    if has_s0:
        (qf, kf, vf, gf, qb, kb, vb, gbk, s0_ref, _, _, of_ref, ob_ref, sfin_ref, st_scr, b_scr) = refs
    else:
        (qf, kf, vf, gf, qb, kb, vb, gbk, of_ref, ob_ref, sfin_ref, st_scr, b_scr) = refs
    n = pl.program_id(1)

    @pl.when(n == 0)
    def _():
        for d in range(2):
            for h in range(H_C):
                st_scr[d, h] = s0_ref[0, d, h].T if has_s0 else jnp.zeros((DV_C, DK_C), F32)

    _gla_direction(qf, kf, vf, gf, st_scr, of_ref, b_scr.at[0], 0, False)
    _gla_direction(qb, kb, vb, gbk, st_scr, ob_ref, b_scr.at[1], 1, True)

    @pl.when(n == pl.num_programs(1) - 1)
    def _():
        for d in range(2):
            for h in range(H_C):
                sfin_ref[0, d, h] = st_scr[d, h].T


def _gla(q, k, v, glog, s0, prev, nb, t, tok0):
    c = GLA_C
    nc = t // c
    b0 = tok0 // c
    kw, vw = H_C * DK_C, H_C * DV_C
    fwd = lambda b, n: (b0 + b * nc + n, 0)
    bwd = lambda b, n: (b0 + b * nc + nc - 1 - n, 0)
    bwd_g = lambda b, n: (b0 + b * nc + nc - 1 - n, 1)
    st_spec = pl.BlockSpec((1, 2, H_C, DK_C, DV_C), lambda b, n: (b, 0, 0, 0, 0))
    in_specs = [pl.BlockSpec((c, kw), fwd), pl.BlockSpec((c, kw), fwd), pl.BlockSpec((c, vw), fwd),
                pl.BlockSpec((c, kw), fwd),
                pl.BlockSpec((c, kw), bwd), pl.BlockSpec((c, kw), bwd), pl.BlockSpec((c, vw), bwd),
                pl.BlockSpec((c, kw), bwd_g)]
    args = [q, k, v, glog, q, k, v, glog]
    aliases = {}
    if s0 is not None:
        in_specs += [st_spec, pl.BlockSpec(memory_space=pl.ANY), pl.BlockSpec(memory_space=pl.ANY)]
        args += [s0, prev[0], prev[1]]
        aliases = {9: 0, 10: 1}
    ntok = q.shape[0]
    return pl.pallas_call(
        functools.partial(_gla_kernel, has_s0=s0 is not None),
        grid=(nb, nc),
        in_specs=in_specs,
        out_specs=[pl.BlockSpec((c, vw), fwd), pl.BlockSpec((c, vw), bwd), st_spec],
        out_shape=[jax.ShapeDtypeStruct((ntok, vw), F32), jax.ShapeDtypeStruct((ntok, vw), F32),
                   jax.ShapeDtypeStruct((nb, 2, H_C, DK_C, DV_C), F32)],
        scratch_shapes=[pltpu.VMEM((2, H_C, DV_C, DK_C), F32), pltpu.VMEM((2, c, kw), F32)],
        input_output_aliases=aliases,
        compiler_params=_cparams(("arbitrary", "arbitrary")),
        name="gla_scan",
    )(*args)


def _pad_cols(w, width):
    return jnp.pad(w, ((0, 0), (0, width - w.shape[1])))


def _swap16(w):
    q = DR_B // 4
    return jnp.concatenate([w[..., q:2 * q], w[..., :q], w[..., 3 * q:], w[..., 2 * q:3 * q]], axis=-1)


def _even_weights(w_in, w_uq, w_ukv):
    o = QKV_A + H_A * DV_A
    a = w_in[:, o:o + 2 * H_A]
    b = w_in[:, o + 2 * H_A:o + 4 * H_A]
    o += 4 * H_A
    cq = w_in[:, o:o + Q_LORA]
    ckv = w_in[:, o + Q_LORA:o + Q_LORA + KV_LORA]
    kr = w_in[:, o + Q_LORA + KV_LORA:]
    w1 = jnp.concatenate([w_in[:, :QKV_A + H_A * DV_A], cq, ckv, _pad_cols(kr, LANES),
                          _pad_cols(_swap16(kr), LANES), _pad_cols(jnp.concatenate([a, b], axis=1), LANES)],
                         axis=1).astype(BF16)
    uq = w_uq.reshape(Q_LORA, H_B, DN_B + DR_B)
    nope, rope = uq[..., :DN_B], uq[..., DN_B:]
    zeros = jnp.zeros_like(rope)
    main = jnp.concatenate([nope, rope, zeros], axis=-1).reshape(Q_LORA, H_B * 2 * LANES)
    swapped = jnp.concatenate([_swap16(rope), zeros], axis=-1).reshape(Q_LORA, H_B * LANES)
    wuq = jnp.concatenate([main, swapped], axis=1).astype(BF16)
    return w1, wuq, w_ukv.astype(BF16)


def _odd_weights(w_in, w_gup, b_g):
    w1 = _pad_cols(w_in, OD_COLS).astype(BF16)
    kq = H_C * DK_C
    wg = jnp.zeros((LANES, 2 * kq), F32)
    wg = wg.at[:GATE_RANK, :kq].set(w_gup[0]).at[GATE_RANK:2 * GATE_RANK, kq:].set(w_gup[1])
    return w1, wg.astype(BF16), b_g.reshape(1, 2 * kq)


def _rope_tables(t_l, tm):
    nf = DR_B // 4
    pos = jnp.arange(t_l)
    row = (pos // GRID_W).astype(F32)[:, None]
    col = (pos % GRID_W).astype(F32)[:, None]
    inv = ROPE_BASE ** (-jnp.arange(nf, dtype=F32) / nf)
    ar, ac = row * inv, col * inv
    zeros = jnp.zeros((t_l, LANES - DR_B), F32)
    cos = jnp.concatenate([jnp.cos(ar), jnp.cos(ar), jnp.cos(ac), jnp.cos(ac), zeros], axis=1)
    sin = jnp.concatenate([-jnp.sin(ar), jnp.sin(ar), -jnp.sin(ac), jnp.sin(ac), zeros], axis=1)
    cos = jnp.concatenate([jnp.ones((tm, LANES), F32), cos], axis=0)
    sin = jnp.concatenate([jnp.zeros((tm, LANES), F32), sin], axis=0)
    return cos, sin


def kernel(x_prompt, x_sample, cache_mla_ckv, cache_mla_krope, state_gdn, state_gla, c, c_ctx, mod_w, mod_b, norm_pre, norm_post, ffn_w_in, ffn_w_out, ev_w_in, ev_conv_w, ev_gdn_a_log, ev_gdn_dt_bias, ev_gdn_norm, ev_mla_q_norm, ev_mla_w_uq, ev_mla_kv_norm, ev_mla_w_ukv, ev_w_out, od_w_in, od_gla_w_gup, od_gla_b_g, od_gla_norm, od_w_out):
    b_c, t_c, d = x_prompt.shape
    b_l, t_l, _ = x_sample.shape
    depth = mod_w.shape[0]
    geo = _Geo(b_c, t_c, b_l, t_l)
    x = jnp.concatenate([x_prompt.reshape(geo.n_c, d), x_sample.reshape(b_l * t_l, d)], axis=0)

    cond8 = jnp.zeros((SUBLANES, d), F32).at[0].set(c_ctx).at[1:1 + b_l].set(c)
    mods_all = _adaln(cond8, mod_w, mod_b).reshape(depth, SUBLANES, N_MOD, d)
    tm_e = geo.tile(256)
    cos_t, sin_t = _rope_tables(t_l, tm_e)

    new_ckv, new_krope, new_gdn, new_gla = [], [], [], []
    for i in range(depth):
        mods = mods_all[i]
        w_in_b, w_out_b = ffn_w_in[i].astype(BF16), ffn_w_out[i].astype(BF16)
        x = _ffn(geo, x, mods, norm_pre[i, 0], norm_post[i, 0], w_in_b[0], w_out_b[0], 0)
        if i % 2 == 0:
            e = i // 2
            w1, wuq, wukv = _even_weights(ev_w_in[e], ev_mla_w_uq[e], ev_mla_w_ukv[e])
            pad8 = lambda a: _pad_cols(a.reshape(1, -1), LANES)
            qkv_pre, z, gb, q, ckv, kr, k, v = _evproj(
                geo, x, mods, norm_pre[i, 1], w1, ev_mla_q_norm[e], wuq, ev_mla_kv_norm[e], wukv,
                pad8(ev_gdn_a_log[e]), pad8(ev_gdn_dt_bias[e]), cos_t, sin_t, tm_e)
            qkv = _conv(geo, qkv_pre, ev_conv_w[e])
            o_f, o_b, s_ctx = _gdn(qkv, gb, None, None, b_c, t_c, 0)
            o_f, o_b, _ = _gdn(qkv, gb, state_gdn[:, e], (o_f, o_b), b_l, t_l, geo.n_c)
            kc, vc = _cachekv(cache_mla_ckv[:, e], jnp.pad(cache_mla_krope[:, e], ((0, 0), (0, 0), (0, LANES - DR_B))),
                              wukv)
            o_att = _attn_ctx(geo, q, k, v)
            o_att = _attn_lat(geo, q, k, v, kc, vc, o_att)
            x = _mixout(geo, o_f, o_b, z, o_att, x, mods, ev_gdn_norm[e], norm_post[i, 1],
                        ev_w_out[e].astype(BF16), H_A, DV_A)
            new_ckv.append(ckv[:geo.n_c].reshape(b_c, t_c, KV_LORA))
            new_krope.append(kr[:geo.n_c].reshape(b_c, t_c, DR_B))
            new_gdn.append(s_ctx)
        else:
            o = i // 2
            w1, wg, bg = _odd_weights(od_w_in[o], od_gla_w_gup[o], od_gla_b_g[o])
            q, k, v, r, glog = _odproj(geo, x, mods, norm_pre[i, 1], w1, wg, bg)
            o_f, o_b, s_ctx = _gla(q, k, v, glog, None, None, b_c, t_c, 0)
            o_f, o_b, _ = _gla(q, k, v, glog, state_gla[:, o], (o_f, o_b), b_l, t_l, geo.n_c)
            x = _mixout(geo, o_f, o_b, r, None, x, mods, od_gla_norm[o], norm_post[i, 1],
                        od_w_out[o].astype(BF16), H_C, DV_C)
            new_gla.append(s_ctx)
        x = _ffn(geo, x, mods, norm_pre[i, 2], norm_post[i, 2], w_in_b[1], w_out_b[1], 6)
    return (x[:geo.n_c].reshape(b_c, t_c, d), x[geo.n_c:].reshape(b_l, t_l, d),
            jnp.stack(new_ckv, axis=1), jnp.stack(new_krope, axis=1),
            jnp.stack(new_gdn, axis=1), jnp.stack(new_gla, axis=1))
```
